```python
import jax, jax.numpy as jnp
from jax import lax
import numpy as np

D_MODEL = 1024
BATCH = 32
SEQ = 2048
DEPTH = 1
DEC_BATCH = 16
DEC_SEQ = 32
PAST_LEN = 1024

CHUNK = 64
CONV_W = 4
D_LRU = D_MODEL
LRU_HEADS = 8
LRU_BLOCK = D_LRU // LRU_HEADS
LRU_C = 8.0
D_SSD = D_MODEL
SSD_HEADS = 16
SSD_HEAD_DIM = D_SSD // SSD_HEADS
SSD_GROUPS = 2
SSD_HPG = SSD_HEADS // SSD_GROUPS
SSD_STATE = 128
D_XBC = D_SSD + 2 * SSD_GROUPS * SSD_STATE
D_MIX = D_LRU + D_SSD
IN_COLS = 2 * D_LRU + D_XBC + D_SSD + SSD_HEADS
EPS = 1e-6

kernel_name = "hawk_ssd_parallel_streaming_encoder_step"


def rmsnorm(x, g):
    xf = x.astype(jnp.float32)
    y = xf * lax.rsqrt(jnp.mean(xf * xf, axis=-1, keepdims=True) + EPS)
    return (y * g.astype(jnp.float32)).astype(x.dtype)


def causal_conv(u, buf, w, b):
    L = u.shape[1]
    ext = jnp.concatenate([buf.astype(u.dtype), u], axis=1)
    out = b + sum(ext[:, k:k + L] * w[k] for k in range(CONV_W))
    return out, ext[:, L:]


def block_diag(u, w, b):
    Bsz, L, _ = u.shape
    uh = u.reshape(Bsz, L, LRU_HEADS, LRU_BLOCK)
    return jnp.einsum('blhi,hij->blhj', uh, w).reshape(Bsz, L, D_LRU) + b


def rg_lru(u, h0, w_a, b_a, w_x, b_x, lam, start):
    L = u.shape[1]
    uf = u.astype(jnp.float32)
    r = jax.nn.sigmoid(block_diag(uf, w_a.astype(jnp.float32), b_a.astype(jnp.float32)))
    i = jax.nn.sigmoid(block_diag(uf, w_x.astype(jnp.float32), b_x.astype(jnp.float32)))
    log_a = -LRU_C * r * jax.nn.softplus(-lam.astype(jnp.float32))
    a = jnp.exp(log_a)
    mult = jnp.sqrt(-jnp.expm1(2.0 * log_a))
    reset = (start + jnp.arange(L)) == 0
    mult = jnp.where(reset[None, :, None], 1.0, mult)
    bv = mult * (i * uf)
    bv = bv.at[:, 0].add(a[:, 0] * h0.astype(jnp.float32))

    def combine(p, q):
        return p[0] * q[0], q[0] * p[1] + q[1]

    _, h = lax.associative_scan(combine, (a, bv), axis=1)
    return h, h[:, -1]


def segsum(x):
    T = x.shape[-1]
    xr = jnp.broadcast_to(x[..., :, None], x.shape + (T,))
    xr = jnp.where(jnp.tril(jnp.ones((T, T), bool), -1), xr, 0.0)
    s = jnp.cumsum(xr, axis=-2)
    return jnp.where(jnp.tril(jnp.ones((T, T), bool)), s, -jnp.inf)


def ssd_scan(x, dA, Bm, Cm, s0, q):
    b_, l = x.shape[:2]
    c = l // q
    x = x.reshape(b_, c, q, SSD_GROUPS, SSD_HPG, SSD_HEAD_DIM)
    Bm = Bm.reshape(b_, c, q, SSD_GROUPS, SSD_STATE)
    Cm = Cm.reshape(b_, c, q, SSD_GROUPS, SSD_STATE)
    A = dA.reshape(b_, c, q, SSD_GROUPS, SSD_HPG).transpose(0, 3, 4, 1, 2)
    A_cs = jnp.cumsum(A, axis=-1)
    Lm = jnp.exp(segsum(A))
    CB = jnp.einsum('bclgn,bcsgn->bcgls', Cm, Bm)
    y_diag = jnp.einsum('bcgls,bgkcls,bcsgkp->bclgkp', CB, Lm, x)
    decay_states = jnp.exp(A_cs[..., -1:] - A_cs)
    states = jnp.einsum('bcsgn,bgkcs,bcsgkp->bcgkpn', Bm, decay_states, x)
    states = jnp.concatenate([s0[:, None], states], axis=1)
    chunk_A = jnp.pad(A_cs[..., -1], ((0, 0), (0, 0), (0, 0), (1, 0)))
    decay_chunk = jnp.exp(segsum(chunk_A))
    new_states = jnp.einsum('bgkzc,bcgkpn->bzgkpn', decay_chunk, states)
    start_states, final = new_states[:, :-1], new_states[:, -1]
    y_off = jnp.einsum('bclgn,bcgkpn,bgkcl->bclgkp', Cm, start_states, jnp.exp(A_cs))
    y = (y_diag + y_off).reshape(b_, l, SSD_GROUPS, SSD_HPG, SSD_HEAD_DIM)
    return y, final


def mixer_layer(x, c, buf_lru, h_lru, buf_ssd, s_ssd, start, q, p):
    Bsz, L, _ = x.shape
    mod = jax.nn.silu(c) @ p['w_ada'] + p['b_ada']
    shift, scale, gate = jnp.split(mod, 3, axis=-1)
    hn = rmsnorm(x, p['norm_g']) * (1.0 + scale[:, None]) + shift[:, None]
    proj = hn @ p['w_in']
    lru_x, lru_g, xbc, z, dt_raw = jnp.split(
        proj, [D_LRU, 2 * D_LRU, 2 * D_LRU + D_XBC, 2 * D_LRU + D_XBC + D_SSD], axis=-1)

    u, new_buf_lru = causal_conv(lru_x, buf_lru, p['lru_conv_w'], p['lru_conv_b'])
    h, new_h = rg_lru(u, h_lru, p['lru_w_a'], p['lru_b_a'], p['lru_w_x'], p['lru_b_x'],
                      p['lru_lambda'], start)
    y_lru = h.astype(x.dtype) * jax.nn.silu(lru_g)

    xbc_c, new_buf_ssd = causal_conv(xbc, buf_ssd, p['ssd_conv_w'], p['ssd_conv_b'])
    xbc_c = jax.nn.silu(xbc_c).astype(jnp.float32)
    xs, Bm, Cm = jnp.split(xbc_c, [D_SSD, D_SSD + SSD_GROUPS * SSD_STATE], axis=-1)
    xs = xs.reshape(Bsz, L, SSD_GROUPS, SSD_HPG, SSD_HEAD_DIM)
    Bm = Bm.reshape(Bsz, L, SSD_GROUPS, SSD_STATE)
    Cm = Cm.reshape(Bsz, L, SSD_GROUPS, SSD_STATE)
    dt = jax.nn.softplus(dt_raw.astype(jnp.float32) + p['ssd_dt_bias'].astype(jnp.float32))
    dt = dt.reshape(Bsz, L, SSD_GROUPS, SSD_HPG)
    A = -jnp.exp(p['ssd_a_log'].astype(jnp.float32)).reshape(SSD_GROUPS, SSD_HPG)
    y, new_s = ssd_scan(xs * dt[..., None], dt * A, Bm, Cm, s_ssd.astype(jnp.float32), q)
    y = y + p['ssd_d'].astype(jnp.float32).reshape(SSD_GROUPS, SSD_HPG)[:, :, None] * xs
    yz = y.reshape(Bsz, L, D_SSD) * jax.nn.silu(z.astype(jnp.float32))
    yz = yz.reshape(Bsz, L, SSD_GROUPS, D_SSD // SSD_GROUPS)
    y_ssd = rmsnorm(yz, p['ssd_norm_g'].reshape(SSD_GROUPS, D_SSD // SSD_GROUPS))
    y_ssd = y_ssd.reshape(Bsz, L, D_SSD).astype(x.dtype)

    mix = jnp.concatenate([y_lru, y_ssd], axis=-1) @ p['w_out']
    out = x + gate[:, None] * mix
    return (out, new_buf_lru, new_h.astype(x.dtype), new_buf_ssd, new_s.astype(x.dtype))


def setup_inputs(seed: int = 0) -> dict:
    key = jax.random.key(seed)
    ks = jax.random.split(key, 32)
    nrm = lambda k, s, sc: jax.random.normal(k, s, jnp.float32) * sc
    a0 = jax.random.uniform(ks[20], (DEPTH, D_LRU), jnp.float32, 0.9, 0.999)
    dt0 = jnp.exp(jax.random.uniform(ks[21], (DEPTH, SSD_HEADS), jnp.float32,
                                     np.log(1e-3), np.log(1e-1)))
    return {
        'x_prompt': nrm(ks[0], (BATCH, SEQ, D_MODEL), 1.0),
        'x_sample': nrm(ks[1], (DEC_BATCH, DEC_SEQ, D_MODEL), 1.0),
        'c_prompt': nrm(ks[2], (BATCH, D_MODEL), 1.0),
        'c_sample': nrm(ks[3], (DEC_BATCH, D_MODEL), 1.0),
        'state_lru_conv': nrm(ks[4], (DEPTH, DEC_BATCH, CONV_W - 1, D_LRU), 1.0),
        'state_lru_h': nrm(ks[5], (DEPTH, DEC_BATCH, D_LRU), 1.0),
        'state_ssd_conv': nrm(ks[6], (DEPTH, DEC_BATCH, CONV_W - 1, D_XBC), 1.0),
        'state_ssd': nrm(ks[7], (DEPTH, DEC_BATCH, SSD_GROUPS, SSD_HPG, SSD_HEAD_DIM, SSD_STATE), 0.1),
        'norm_g': 1.0 + nrm(ks[8], (DEPTH, D_MODEL), 0.02),
        'w_ada': nrm(ks[9], (DEPTH, D_MODEL, 3 * D_MODEL), 0.5 * D_MODEL ** -0.5),
        'b_ada': nrm(ks[10], (DEPTH, 3 * D_MODEL), 0.01),
        'w_in': nrm(ks[11], (DEPTH, D_MODEL, IN_COLS), D_MODEL ** -0.5),
        'lru_conv_w': nrm(ks[12], (DEPTH, CONV_W, D_LRU), CONV_W ** -0.5),
        'lru_conv_b': nrm(ks[13], (DEPTH, D_LRU), 0.01),
        'lru_w_a': nrm(ks[14], (DEPTH, LRU_HEADS, LRU_BLOCK, LRU_BLOCK), LRU_BLOCK ** -0.5),
        'lru_b_a': nrm(ks[15], (DEPTH, D_LRU), 0.01),
        'lru_w_x': nrm(ks[16], (DEPTH, LRU_HEADS, LRU_BLOCK, LRU_BLOCK), LRU_BLOCK ** -0.5),
        'lru_b_x': nrm(ks[17], (DEPTH, D_LRU), 0.01),
        'lru_lambda': jnp.log(a0) - jnp.log1p(-a0),
        'ssd_conv_w': nrm(ks[18], (DEPTH, CONV_W, D_XBC), CONV_W ** -0.5),
        'ssd_conv_b': nrm(ks[19], (DEPTH, D_XBC), 0.01),
        'ssd_dt_bias': dt0 + jnp.log(-jnp.expm1(-dt0)),
        'ssd_a_log': jnp.log(jax.random.uniform(ks[22], (DEPTH, SSD_HEADS), jnp.float32, 1.0, 16.0)),
        'ssd_d': 1.0 + nrm(ks[23], (DEPTH, SSD_HEADS), 0.1),
        'ssd_norm_g': 1.0 + nrm(ks[24], (DEPTH, D_SSD), 0.02),
        'w_out': nrm(ks[25], (DEPTH, D_MIX, D_MODEL), D_MIX ** -0.5),
        'final_norm_g': 1.0 + nrm(ks[26], (D_MODEL,), 0.02),
    }


def reference(x_prompt, x_sample, c_prompt, c_sample, state_lru_conv, state_lru_h,
              state_ssd_conv, state_ssd, norm_g, w_ada, b_ada, w_in, lru_conv_w, lru_conv_b,
              lru_w_a, lru_b_a, lru_w_x, lru_b_x, lru_lambda, ssd_conv_w, ssd_conv_b,
              ssd_dt_bias, ssd_a_log, ssd_d, ssd_norm_g, w_out, final_norm_g):
    xp, xs = x_prompt, x_sample
    bp = xp.shape[0]
    lc_p, lh_p, sc_p, ss_p = [], [], [], []
    lc_s, lh_s, sc_s, ss_s = [], [], [], []
    for l in range(DEPTH):
        p = {'norm_g': norm_g[l], 'w_ada': w_ada[l], 'b_ada': b_ada[l], 'w_in': w_in[l],
             'lru_conv_w': lru_conv_w[l], 'lru_conv_b': lru_conv_b[l],
             'lru_w_a': lru_w_a[l], 'lru_b_a': lru_b_a[l], 'lru_w_x': lru_w_x[l],
             'lru_b_x': lru_b_x[l], 'lru_lambda': lru_lambda[l],
             'ssd_conv_w': ssd_conv_w[l], 'ssd_conv_b': ssd_conv_b[l],
             'ssd_dt_bias': ssd_dt_bias[l], 'ssd_a_log': ssd_a_log[l], 'ssd_d': ssd_d[l],
             'ssd_norm_g': ssd_norm_g[l], 'w_out': w_out[l]}
        xp, a1, a2, a3, a4 = mixer_layer(
            xp, c_prompt,
            jnp.zeros((bp, CONV_W - 1, D_LRU), xp.dtype), jnp.zeros((bp, D_LRU), xp.dtype),
            jnp.zeros((bp, CONV_W - 1, D_XBC), xp.dtype),
            jnp.zeros((bp, SSD_GROUPS, SSD_HPG, SSD_HEAD_DIM, SSD_STATE), xp.dtype),
            0, CHUNK, p)
        xs, b1, b2, b3, b4 = mixer_layer(
            xs, c_sample, state_lru_conv[l], state_lru_h[l], state_ssd_conv[l], state_ssd[l],
            PAST_LEN, xs.shape[1], p)
        lc_p.append(a1); lh_p.append(a2); sc_p.append(a3); ss_p.append(a4)
        lc_s.append(b1); lh_s.append(b2); sc_s.append(b3); ss_s.append(b4)
    y_prompt = rmsnorm(xp, final_norm_g)
    y_sample = rmsnorm(xs, final_norm_g)
    return (y_prompt, y_sample,
            jnp.stack(lc_p), jnp.stack(lh_p), jnp.stack(sc_p), jnp.stack(ss_p),
            jnp.stack(lc_s), jnp.stack(lh_s), jnp.stack(sc_s), jnp.stack(ss_s))
```

```python
import functools

import jax
import jax.numpy as jnp
from jax import lax
from jax.experimental import pallas as pl
from jax.experimental.pallas import tpu as pltpu

F32 = jnp.float32
BF16 = jnp.bfloat16

CONV_W = 4
LRU_HEADS = 8
LRU_C = 8.0
SSD_HEADS = 16
SSD_GROUPS = 2
SSD_STATE = 128
CHUNK = 64
PAST_LEN = 1024
EPS = 1e-6

LANES = 128
SUBLANES = 8
VMEM_LIMIT_BYTES = 56 * 1024 * 1024

NEG_BIG = -1e30


def _dot(a, b):
    return jnp.dot(a, b, preferred_element_type=F32)


def _dot_nt(a, b):
    return lax.dot_general(a, b, (((1,), (1,)), ((), ())), preferred_element_type=F32)


def _dot_tn(a, b):
    return lax.dot_general(a, b, (((0,), (0,)), ((), ())), preferred_element_type=F32)


def _split3(x):
    hi = x.astype(BF16)
    r1 = x - hi.astype(F32)
    mid = r1.astype(BF16)
    lo = (r1 - mid.astype(F32)).astype(BF16)
    return hi, mid, lo


def _dot_sel_rhs(sel, x):
    hi, mid, lo = _split3(x)
    return _dot(sel, hi) + _dot(sel, mid) + _dot(sel, lo)


def _dot_sel_lhs(x, sel):
    hi, mid, lo = _split3(x)
    return _dot(hi, sel) + _dot(mid, sel) + _dot(lo, sel)


def _sigmoid(x):
    return 0.5 * jnp.tanh(0.5 * x) + 0.5


def _silu(x):
    return x * _sigmoid(x)


def _softplus(x):
    return jnp.maximum(x, 0.0) + jnp.log1p(jnp.exp(-jnp.abs(x)))


def _adaln_body(c_ref, w_ref, b_ref, o_ref):
    a = _silu(c_ref[...]).astype(BF16)
    o_ref[0] = _dot(a, w_ref[...].astype(BF16)) + b_ref[...]


def _adaln(c_all, w_ada, b_ada):
    rows, d = c_all.shape
    parts = w_ada.shape[1] // d
    return pl.pallas_call(
        _adaln_body,
        grid=(parts,),
        in_specs=[
            pl.BlockSpec((rows, d), lambda j: (0, 0)),
            pl.BlockSpec((d, d), lambda j: (0, j)),
            pl.BlockSpec((1, d), lambda j: (0, j)),
        ],
        out_specs=pl.BlockSpec((1, rows, d), lambda j: (j, 0, 0)),
        out_shape=jax.ShapeDtypeStruct((parts, rows, d), F32),
        name="adaln",
    )(c_all, w_ada, b_ada.reshape(1, -1))


def _mixer_body(cfg, *refs):
    nb, tl, q, start, has_init, d, dl, ds, dxbc, zw = cfg
    hd = ds // SSD_HEADS
    gw = ds // SSD_GROUPS
    lb = dl // LRU_HEADS
    hpg = SSD_HEADS // SSD_GROUPS
    n_chunks = tl // q
    rows = nb * tl

    it = iter(refs)
    x_ref = next(it)
    mod_ref = next(it)
    if has_init:
        lc0_ref, lh0_ref, sc0_ref, ss0_ref = next(it), next(it), next(it), next(it)
    (ng_ref, win_ref, lcw_ref, lcb_ref, wg_ref, ba_ref, bx_ref, lam_ref, scw_ref, scb_ref,
     dtb_ref, alog_ref, dsk_ref, sng_ref, wout_ref, fng_ref) = [next(it) for _ in range(16)]
    y_ref, lco_ref, lho_ref, sco_ref, sso_ref = [next(it) for _ in range(5)]
    lext, sext, gbuf, zbuf, mix, hc, st = [next(it) for _ in range(7)]

    l_idx = pl.program_id(1)
    n_l = pl.num_programs(1)

    @pl.when(l_idx == 0)
    def _():
        if has_init:
            lext[:, 0:SUBLANES, :] = lc0_ref[...]
            sext[:, 0:SUBLANES, :] = sc0_ref[...]
            hc[...] = lh0_ref[...]
            for n in range(nb):
                for g in range(SSD_GROUPS):
                    st[n, g] = ss0_ref[n, g * gw:(g + 1) * gw, :].T
        else:
            lext[:, 0:SUBLANES, :] = jnp.zeros((nb, SUBLANES, dl), F32)
            sext[:, 0:SUBLANES, :] = jnp.zeros((nb, SUBLANES, dxbc), F32)
            hc[...] = jnp.zeros(hc.shape, F32)
            st[...] = jnp.zeros(st.shape, F32)

    x = x_ref[...]
    ms = jnp.mean(x * x, axis=-1, keepdims=True)
    hn = (x * lax.rsqrt(ms + EPS) * ng_ref[...]) * (1.0 + mod_ref[1]) + mod_ref[0]
    hb = hn.reshape(rows, d).astype(BF16)
    lext[:, SUBLANES:SUBLANES + tl, :] = _dot(hb, win_ref[:, 0:dl]).reshape(nb, tl, dl)
    gbuf[...] = _dot(hb, win_ref[:, dl:2 * dl])
    sext[:, SUBLANES:SUBLANES + tl, :] = _dot(hb, win_ref[:, 2 * dl:2 * dl + dxbc]).reshape(nb, tl, dxbc)
    zbuf[...] = _dot(hb, win_ref[:, 2 * dl + dxbc:2 * dl + dxbc + zw])

    cneg = -LRU_C * _softplus(-lam_ref[...])
    a_neg = -jnp.exp(alog_ref[...])
    r_qq = lax.broadcasted_iota(jnp.int32, (q, q), 0)
    c_qq = lax.broadcasted_iota(jnp.int32, (q, q), 1)
    tril = jnp.where(r_qq >= c_qq, 1.0, 0.0).astype(BF16)
    e_r = lax.broadcasted_iota(jnp.int32, (LANES, ds), 0)
    e_c = lax.broadcasted_iota(jnp.int32, (LANES, ds), 1)
    exp_p = jnp.where(e_r == e_c // hd, 1.0, 0.0).astype(BF16)
    if q == hd:
        exp_q = exp_p
    else:
        eq_r = lax.broadcasted_iota(jnp.int32, (LANES, SSD_HEADS * q), 0)
        eq_c = lax.broadcasted_iota(jnp.int32, (LANES, SSD_HEADS * q), 1)
        exp_q = jnp.where(eq_r == eq_c // q, 1.0, 0.0).astype(BF16)
    row_t = lax.broadcasted_iota(jnp.int32, (q, SSD_HEADS * q), 0)
    lane_t = lax.broadcasted_iota(jnp.int32, (q, SSD_HEADS * q), 1) % q
    eye_t = lane_t == row_t
    causal_t = lane_t <= row_t
    bd_r = lax.broadcasted_iota(jnp.int32, (4 * q, 4 * hd), 0) // q
    bd_c = lax.broadcasted_iota(jnp.int32, (4 * q, 4 * hd), 1) // hd
    bd_mask = jnp.where(bd_r == bd_c, 1.0, 0.0).astype(BF16)
    sub_rows = lax.broadcasted_iota(jnp.int32, (SUBLANES, lb), 0)
    first_row = lax.broadcasted_iota(jnp.int32, (q, lb), 0) == 0

    for n in range(nb):
        for c in range(n_chunks):
            r0 = c * q
            g0 = n * tl + r0

            u = lcb_ref[...]
            for k in range(CONV_W):
                u = u + lcw_ref[k:k + 1, :] * lext[n, r0 + 5 + k:r0 + 5 + k + q, :]
            ub = u.astype(BF16)
            for h in range(LRU_HEADS):
                hs = slice(h * lb, (h + 1) * lb)
                u_h = u[:, hs]
                gates = _dot(ub[:, hs], wg_ref[h])
                r_g = _sigmoid(gates[:, :lb] + ba_ref[:, hs])
                i_g = _sigmoid(gates[:, lb:] + bx_ref[:, hs])
                log_a = cneg[:, hs] * r_g
                a = jnp.exp(log_a)
                mult = jnp.sqrt(-jnp.tanh(log_a) * (a * a + 1.0))
                if start == 0 and c == 0:
                    mult = jnp.where(jnp.logical_and(first_row, l_idx == 0), 1.0, mult)
                bv = mult * (i_g * u_h)
                hp = hc[n, :, hs]
                blocks = []
                for j in range(q // SUBLANES):
                    a_b = a[j * SUBLANES:(j + 1) * SUBLANES]
                    b_b = bv[j * SUBLANES:(j + 1) * SUBLANES]
                    for k in (1, 2, 4):
                        keep = sub_rows >= k
                        a_s = jnp.where(keep, pltpu.roll(a_b, k, 0), 1.0)
                        b_s = jnp.where(keep, pltpu.roll(b_b, k, 0), 0.0)
                        b_b = a_b * b_s + b_b
                        a_b = a_b * a_s
                    h_b = a_b * hp + b_b
                    hp = h_b[SUBLANES - 1:SUBLANES]
                    blocks.append(h_b)
                hc[n, :, hs] = hp
                h_all = jnp.concatenate(blocks, axis=0)
                y_l = h_all * _silu(gbuf[g0:g0 + q, hs])
                mix[g0:g0 + q, hs] = y_l.astype(BF16)

            xbc = scb_ref[...]
            for k in range(CONV_W):
                xbc = xbc + scw_ref[k:k + 1, :] * sext[n, r0 + 5 + k:r0 + 5 + k + q, :]
            xbc = _silu(xbc)
            xs = xbc[:, :ds]
            bm = xbc[:, ds:ds + SSD_GROUPS * SSD_STATE].astype(BF16)
            cm = xbc[:, ds + SSD_GROUPS * SSD_STATE:].astype(BF16)
            zd = zbuf[g0:g0 + q, :]
            z = zd[:, :ds]
            dt = _softplus(zd[:, ds:ds + LANES] + dtb_ref[...])
            d_a = dt * a_neg
            cs = _dot_sel_rhs(tril, d_a)
            rp = _dot_sel_lhs(cs, exp_p)
            rq = rp if q == hd else _dot_sel_lhs(cs, exp_q)
            crow = jnp.sum(jnp.where(eye_t, rq, 0.0), axis=0, keepdims=True)
            lm = jnp.exp(jnp.where(causal_t, rq - crow, NEG_BIG))
            xdt = xs * _dot_sel_lhs(dt, exp_p)
            xb = xdt.astype(BF16)
            y_parts = []
            yo_parts = []
            for g in range(SSD_GROUPS):
                bm_g = bm[:, g * SSD_STATE:(g + 1) * SSD_STATE]
                cm_g = cm[:, g * SSD_STATE:(g + 1) * SSD_STATE]
                cb_t = _dot_nt(cm_g, jnp.concatenate([bm_g] * hpg, axis=0))
                m_g = (lm[:, g * hpg * q:(g + 1) * hpg * q] * cb_t).astype(BF16)
                for qd in range(hpg // 4):
                    kq = g * (hpg // 4) + qd
                    x_q = xb[:, kq * 4 * hd:(kq + 1) * 4 * hd]
                    bd = jnp.concatenate([x_q] * 4, axis=0) * bd_mask
                    y_parts.append(_dot(m_g[:, qd * 4 * q:(qd + 1) * 4 * q], bd))
                yo_parts.append(_dot(cm_g, st[n, g].astype(BF16)))
            y = (jnp.concatenate(y_parts, axis=1)
                 + jnp.exp(rp) * jnp.concatenate(yo_parts, axis=1)
                 + dsk_ref[...] * xs)
            yz = y * _silu(z)
            for g in range(SSD_GROUPS):
                y_g = yz[:, g * gw:(g + 1) * gw]
                ms_g = jnp.mean(y_g * y_g, axis=-1, keepdims=True)
                y_n = y_g * lax.rsqrt(ms_g + EPS) * sng_ref[:, g * gw:(g + 1) * gw]
                mix[g0:g0 + q, dl + g * gw:dl + (g + 1) * gw] = y_n.astype(BF16)
            rp_last = rp[q - 1:q, :]
            xw = (xdt * jnp.exp(rp_last - rp)).astype(BF16)
            for g in range(SSD_GROUPS):
                bm_g = bm[:, g * SSD_STATE:(g + 1) * SSD_STATE]
                st[n, g] = (jnp.exp(rp_last[:, g * gw:(g + 1) * gw]) * st[n, g]
                            + _dot_tn(bm_g, xw[:, g * gw:(g + 1) * gw]))

    lext[:, 0:SUBLANES, :] = lext[:, tl:tl + SUBLANES, :]
    sext[:, 0:SUBLANES, :] = sext[:, tl:tl + SUBLANES, :]

    @pl.when(l_idx == n_l - 1)
    def _():
        lco_ref[...] = lext[:, 5:SUBLANES, :]
        sco_ref[...] = sext[:, 5:SUBLANES, :]
        lho_ref[...] = hc[...]
        for n in range(nb):
            for g in range(SSD_GROUPS):
                sso_ref[n, g * gw:(g + 1) * gw, :] = st[n, g].T

    o = _dot(mix[...], wout_ref[...]).reshape(nb, tl, d)
    out = x_ref[...] + mod_ref[2] * o
    ms_o = jnp.mean(out * out, axis=-1, keepdims=True)
    y_ref[...] = out * lax.rsqrt(ms_o + EPS) * fng_ref[...]


def _const_spec(shape):
    nd = len(shape)
    return pl.BlockSpec(shape, lambda b, l: (0,) * nd, pipeline_mode=pl.Buffered(1))


def _mixer(x, mod4, mod_block_off, weights, *, nb, tl, q, start, init=None):
    b, seq, d = x.shape
    (ng, win, lcw, lcb, wg, ba, bx, lam, scw, scb, dtb, alog, dsk, sng, wout, fng) = weights
    dl = lcw.shape[1]
    dxbc = scw.shape[1]
    ds = sng.shape[1]
    zw = win.shape[1] - 2 * dl - dxbc
    gsz = ds // SSD_GROUPS
    has_init = init is not None
    cfg = (nb, tl, q, start, has_init, d, dl, ds, dxbc, zw)
    rows = nb * tl

    in_specs = [
        pl.BlockSpec((nb, tl, d), lambda i, l: (i, l, 0)),
        pl.BlockSpec((3, nb, 1, d), lambda i, l: (0, mod_block_off + i, 0, 0)),
    ]
    args = [x, mod4]
    if has_init:
        lc0, lh0, sc0, ss0 = init
        in_specs += [
            pl.BlockSpec((nb, SUBLANES, dl), lambda i, l: (i, 0, 0)),
            pl.BlockSpec((nb, 1, dl), lambda i, l: (i, 0, 0)),
            pl.BlockSpec((nb, SUBLANES, dxbc), lambda i, l: (i, 0, 0)),
            pl.BlockSpec((nb, ds, SSD_STATE), lambda i, l: (i, 0, 0)),
        ]
        args += [lc0, lh0, sc0, ss0]
    in_specs += [_const_spec(w.shape) for w in weights]
    args += list(weights)

    out_shape = [
        jax.ShapeDtypeStruct((b, seq, d), F32),
        jax.ShapeDtypeStruct((b, CONV_W - 1, dl), F32),
        jax.ShapeDtypeStruct((b, 1, dl), F32),
        jax.ShapeDtypeStruct((b, CONV_W - 1, dxbc), F32),
        jax.ShapeDtypeStruct((b, ds, SSD_STATE), F32),
    ]
    out_specs = [
        pl.BlockSpec((nb, tl, d), lambda i, l: (i, l, 0)),
        pl.BlockSpec((nb, CONV_W - 1, dl), lambda i, l: (i, 0, 0)),
        pl.BlockSpec((nb, 1, dl), lambda i, l: (i, 0, 0)),
        pl.BlockSpec((nb, CONV_W - 1, dxbc), lambda i, l: (i, 0, 0)),
        pl.BlockSpec((nb, ds, SSD_STATE), lambda i, l: (i, 0, 0)),
    ]
    scratch = [
        pltpu.VMEM((nb, SUBLANES + tl, dl), F32),
        pltpu.VMEM((nb, SUBLANES + tl, dxbc), F32),
        pltpu.VMEM((rows, dl), F32),
        pltpu.VMEM((rows, zw), F32),
        pltpu.VMEM((rows, dl + ds), BF16),
        pltpu.VMEM((nb, 1, dl), F32),
        pltpu.VMEM((nb, SSD_GROUPS, SSD_STATE, gsz), F32),
    ]
    return pl.pallas_call(
        functools.partial(_mixer_body, cfg),
        grid=(b // nb, seq // tl),
        in_specs=in_specs,
        out_specs=out_specs,
        out_shape=out_shape,
        scratch_shapes=scratch,
        compiler_params=pltpu.CompilerParams(
            dimension_semantics=("arbitrary", "arbitrary"),
            vmem_limit_bytes=VMEM_LIMIT_BYTES,
        ),
        name="mixer_init" if has_init else "mixer",
    )(*args)


def kernel(x_prompt, x_sample, c_prompt, c_sample, state_lru_conv, state_lru_h, state_ssd_conv, state_ssd, norm_g, w_ada, b_ada, w_in, lru_conv_w, lru_conv_b, lru_w_a, lru_b_a, lru_w_x, lru_b_x, lru_lambda, ssd_conv_w, ssd_conv_b, ssd_dt_bias, ssd_a_log, ssd_d, ssd_norm_g, w_out, final_norm_g):
    depth = w_in.shape[0]
    assert depth == 1, "single-layer step"
    bp, seq_p, d = x_prompt.shape
    bs, seq_s, _ = x_sample.shape
    dl = lru_conv_w.shape[-1]
    dxbc = ssd_conv_w.shape[-1]
    ds = ssd_norm_g.shape[-1]
    hd = ds // SSD_HEADS
    in_cols = w_in.shape[-1]
    in_cols_pad = -(-in_cols // LANES) * LANES
    assert in_cols == 2 * dl + dxbc + ds + SSD_HEADS

    c_all = jnp.concatenate([c_prompt, c_sample], axis=0)
    mod = _adaln(c_all, w_ada[0], b_ada[0])
    mod4 = mod.reshape(mod.shape[0], bp + bs, 1, d)

    row = lambda v: v.reshape(1, -1)
    pad_lanes = lambda v: jnp.pad(v.reshape(1, -1), ((0, 0), (0, LANES - v.size)))
    weights = (
        row(norm_g[0]),
        jnp.pad(w_in[0], ((0, 0), (0, in_cols_pad - in_cols))).astype(BF16),
        lru_conv_w[0], row(lru_conv_b[0]),
        jnp.concatenate([lru_w_a[0], lru_w_x[0]], axis=-1).astype(BF16),
        row(lru_b_a[0]), row(lru_b_x[0]), row(lru_lambda[0]),
        ssd_conv_w[0], row(ssd_conv_b[0]),
        pad_lanes(ssd_dt_bias[0]), pad_lanes(ssd_a_log[0]),
        row(jnp.repeat(ssd_d[0], hd)), row(ssd_norm_g[0]),
        w_out[0].astype(BF16), row(final_norm_g),
    )

    nb_s = 8
    assert bp % nb_s == 0 and bs % nb_s == 0
    tl_p = 4 * CHUNK
    assert seq_p % tl_p == 0
    y_p, lc_p, lh_p, sc_p, ss_p = _mixer(
        x_prompt, mod4, 0, weights, nb=1, tl=tl_p, q=CHUNK, start=0)
    pad_tail = lambda v: jnp.pad(v, ((0, 0), (SUBLANES - (CONV_W - 1), 0), (0, 0)))
    init_s = (
        pad_tail(state_lru_conv[0]),
        state_lru_h[0].reshape(bs, 1, dl),
        pad_tail(state_ssd_conv[0]),
        state_ssd[0].reshape(bs, ds, SSD_STATE),
    )
    y_s, lc_s, lh_s, sc_s, ss_s = _mixer(
        x_sample, mod4, bp // nb_s, weights, nb=nb_s, tl=seq_s, q=seq_s, start=PAST_LEN, init=init_s)

    st_shape = (1,) + state_ssd.shape[2:]
    return (
        y_p, y_s,
        lc_p[None], lh_p.reshape(1, bp, dl), sc_p[None], ss_p.reshape((1, bp) + st_shape[1:]),
        lc_s[None], lh_s.reshape(1, bs, dl), sc_s[None], ss_s.reshape((1, bs) + st_shape[1:]),
    )
```

```python
import functools

import jax
import jax.numpy as jnp
from jax import lax
from jax.experimental import pallas as pl
from jax.experimental.pallas import tpu as pltpu

F32 = jnp.float32
BF16 = jnp.bfloat16

CONV_W = 4
LRU_HEADS = 8
LRU_C = 8.0
SSD_HEADS = 16
SSD_GROUPS = 2
SSD_STATE = 128
CHUNK = 64
PAST_LEN = 1024
EPS = 1e-6

LANES = 128
SUBLANES = 8
VMEM_LIMIT_BYTES = 56 * 1024 * 1024

NEG_BIG = -1e30
LOG2E = 1.4426950408889634
LN2 = 0.6931471805599453


def _dot(a, b):
    return jnp.dot(a, b, preferred_element_type=F32)


def _dot_nt(a, b):
    return lax.dot_general(a, b, (((1,), (1,)), ((), ())), preferred_element_type=F32)


def _dot_tn(a, b):
    return lax.dot_general(a, b, (((0,), (0,)), ((), ())), preferred_element_type=F32)


def _split3(x):
    hi = x.astype(BF16)
    r1 = x - hi.astype(F32)
    mid = r1.astype(BF16)
    lo = (r1 - mid.astype(F32)).astype(BF16)
    return hi, mid, lo


def _dot_sel_rhs(sel, x):
    hi, mid, lo = _split3(x)
    return _dot(sel, hi) + _dot(sel, mid) + _dot(sel, lo)


def _dot_sel_lhs(x, sel):
    hi, mid, lo = _split3(x)
    return _dot(hi, sel) + _dot(mid, sel) + _dot(lo, sel)


def _sigmoid(x):
    return 1.0 / (1.0 + jnp.exp2(x * (-LOG2E)))


def _silu(x):
    hx = 0.5 * x
    return hx * jnp.tanh(hx) + hx


def _causal_conv_tiles(ext_ref, n, r0, q, w_ref, b_ref, sub_rows):
    c = ext_ref.shape[-1]
    tiles = []
    for t in range(c // LANES):
        ls = slice(t * LANES, (t + 1) * LANES)
        w = [w_ref[k:k + 1, ls] for k in range(CONV_W)]
        bias = b_ref[:, ls]
        slab = ext_ref[n, r0:r0 + SUBLANES, ls]
        prev = [pltpu.roll(slab, CONV_W - 1 - k, 0) for k in range(CONV_W - 1)]
        outs = []
        for j in range(q // SUBLANES):
            lo = r0 + (j + 1) * SUBLANES
            slab = ext_ref[n, lo:lo + SUBLANES, ls]
            cur = [pltpu.roll(slab, CONV_W - 1 - k, 0) for k in range(CONV_W - 1)]
            acc = bias + w[CONV_W - 1] * slab
            for k in range(CONV_W - 1):
                acc = acc + w[k] * jnp.where(sub_rows < CONV_W - 1 - k, prev[k], cur[k])
            outs.append(acc)
            prev = cur
        tiles.append(jnp.concatenate(outs, axis=0))
    return tiles


def _softplus(x):
    return jnp.maximum(x, 0.0) + jnp.log1p(jnp.exp(-jnp.abs(x)))


def _adaln_body(c_ref, w_ref, b_ref, o_ref):
    a = _silu(c_ref[...]).astype(BF16)
    o_ref[0] = _dot(a, w_ref[...].astype(BF16)) + b_ref[...]


def _adaln(c_all, w_ada, b_ada):
    rows, d = c_all.shape
    parts = w_ada.shape[1] // d
    return pl.pallas_call(
        _adaln_body,
        grid=(parts,),
        in_specs=[
            pl.BlockSpec((rows, d), lambda j: (0, 0)),
            pl.BlockSpec((d, d), lambda j: (0, j)),
            pl.BlockSpec((1, d), lambda j: (0, j)),
        ],
        out_specs=pl.BlockSpec((1, rows, d), lambda j: (j, 0, 0)),
        out_shape=jax.ShapeDtypeStruct((parts, rows, d), F32),
        name="adaln",
    )(c_all, w_ada, b_ada.reshape(1, -1))


def _mixer_body(cfg, *refs):
    nb, tl, q, start, has_init, d, dl, ds, dxbc, zw = cfg
    hd = ds // SSD_HEADS
    gw = ds // SSD_GROUPS
    lb = dl // LRU_HEADS
    hpg = SSD_HEADS // SSD_GROUPS
    n_chunks = tl // q
    rows = nb * tl

    it = iter(refs)
    x_ref = next(it)
    mod_ref = next(it)
    if has_init:
        lc0_ref, lh0_ref, sc0_ref, ss0_ref = next(it), next(it), next(it), next(it)
    (ng_ref, win_ref, lcw_ref, lcb_ref, wg_ref, ba_ref, bx_ref, lam_ref, scw_ref, scb_ref,
     dtb_ref, alog_ref, dsk_ref, sng_ref, wout_ref, fng_ref) = [next(it) for _ in range(16)]
    y_ref, lco_ref, lho_ref, sco_ref, sso_ref = [next(it) for _ in range(5)]
    lext, sext, gbuf, zbuf, mix, hc, st = [next(it) for _ in range(7)]

    l_idx = pl.program_id(1)
    n_l = pl.num_programs(1)

    @pl.when(l_idx == 0)
    def _():
        if has_init:
            lext[:, 0:SUBLANES, :] = lc0_ref[...]
            sext[:, 0:SUBLANES, :] = sc0_ref[...]
            hc[...] = lh0_ref[...]
            for n in range(nb):
                for g in range(SSD_GROUPS):
                    st[n, g] = ss0_ref[n, g * gw:(g + 1) * gw, :].T
        else:
            lext[:, 0:SUBLANES, :] = jnp.zeros((nb, SUBLANES, dl), F32)
            sext[:, 0:SUBLANES, :] = jnp.zeros((nb, SUBLANES, dxbc), F32)
            hc[...] = jnp.zeros(hc.shape, F32)
            st[...] = jnp.zeros(st.shape, F32)

    x = x_ref[...]
    ms = jnp.mean(x * x, axis=-1, keepdims=True)
    hn = x * lax.rsqrt(ms + EPS) * (ng_ref[...] * (1.0 + mod_ref[1])) + mod_ref[0]
    hb = hn.reshape(rows, d).astype(BF16)
    lext[:, SUBLANES:SUBLANES + tl, :] = _dot(hb, win_ref[:, 0:dl]).reshape(nb, tl, dl)
    gbuf[...] = _dot(hb, win_ref[:, dl:2 * dl])
    sext[:, SUBLANES:SUBLANES + tl, :] = _dot(hb, win_ref[:, 2 * dl:2 * dl + dxbc]).reshape(nb, tl, dxbc)
    zbuf[...] = _dot(hb, win_ref[:, 2 * dl + dxbc:2 * dl + dxbc + zw])

    cneg2 = (-LRU_C * LOG2E) * _softplus(-lam_ref[...])
    a_neg2 = (-LOG2E) * jnp.exp(alog_ref[...])
    r_qq = lax.broadcasted_iota(jnp.int32, (q, q), 0)
    c_qq = lax.broadcasted_iota(jnp.int32, (q, q), 1)
    tril = jnp.where(r_qq >= c_qq, 1.0, 0.0).astype(BF16)
    e_r = lax.broadcasted_iota(jnp.int32, (LANES, ds), 0)
    e_c = lax.broadcasted_iota(jnp.int32, (LANES, ds), 1)
    exp_p = jnp.where(e_r == e_c // hd, 1.0, 0.0).astype(BF16)
    if q == hd:
        exp_q = exp_p
    else:
        eq_r = lax.broadcasted_iota(jnp.int32, (LANES, SSD_HEADS * q), 0)
        eq_c = lax.broadcasted_iota(jnp.int32, (LANES, SSD_HEADS * q), 1)
        exp_q = jnp.where(eq_r == eq_c // q, 1.0, 0.0).astype(BF16)
    row_t = lax.broadcasted_iota(jnp.int32, (q, SSD_HEADS * q), 0)
    lane_t = lax.broadcasted_iota(jnp.int32, (q, SSD_HEADS * q), 1) % q
    eye_t = lane_t == row_t
    causal_t = lane_t <= row_t
    bd_r = lax.broadcasted_iota(jnp.int32, (4 * q, 4 * hd), 0) // q
    bd_c = lax.broadcasted_iota(jnp.int32, (4 * q, 4 * hd), 1) // hd
    bd_mask = jnp.where(bd_r == bd_c, 1.0, 0.0).astype(BF16)
    sub_rows = lax.broadcasted_iota(jnp.int32, (SUBLANES, lb), 0)
    first_row = lax.broadcasted_iota(jnp.int32, (q, lb), 0) == 0

    for n in range(nb):
        for c in range(n_chunks):
            r0 = c * q
            g0 = n * tl + r0

            u_tiles = _causal_conv_tiles(lext, n, r0, q, lcw_ref, lcb_ref, sub_rows)
            for h in range(LRU_HEADS):
                hs = slice(h * lb, (h + 1) * lb)
                u_h = u_tiles[h]
                gates = _dot(u_h.astype(BF16), wg_ref[h])
                r_g = _sigmoid(gates[:, :lb] + ba_ref[:, hs])
                i_g = _sigmoid(gates[:, lb:] + bx_ref[:, hs])
                log2_a = cneg2[:, hs] * r_g
                a = jnp.exp2(log2_a)
                m2 = jnp.tanh(log2_a * (-LN2)) * (a * a + 1.0)
                mult = jnp.where(m2 > 0.0, m2 * lax.rsqrt(m2), 0.0)
                if start == 0 and c == 0:
                    mult = jnp.where(jnp.logical_and(first_row, l_idx == 0), 1.0, mult)
                bv = mult * (i_g * u_h)
                hp = hc[n, :, hs]
                blocks = []
                for j in range(q // SUBLANES):
                    a_b = a[j * SUBLANES:(j + 1) * SUBLANES]
                    b_b = bv[j * SUBLANES:(j + 1) * SUBLANES]
                    for k in (1, 2, 4):
                        keep = sub_rows >= k
                        a_s = jnp.where(keep, pltpu.roll(a_b, k, 0), 1.0)
                        b_s = jnp.where(keep, pltpu.roll(b_b, k, 0), 0.0)
                        b_b = a_b * b_s + b_b
                        a_b = a_b * a_s
                    h_b = a_b * hp + b_b
                    hp = h_b[SUBLANES - 1:SUBLANES]
                    blocks.append(h_b)
                hc[n, :, hs] = hp
                h_all = jnp.concatenate(blocks, axis=0)
                y_l = h_all * _silu(gbuf[g0:g0 + q, hs])
                mix[g0:g0 + q, hs] = y_l.astype(BF16)

            x_tiles = [_silu(t) for t in _causal_conv_tiles(sext, n, r0, q, scw_ref, scb_ref, sub_rows)]
            n_xs = ds // LANES
            n_bc = SSD_GROUPS * SSD_STATE // LANES
            xs = jnp.concatenate(x_tiles[:n_xs], axis=1)
            bm = jnp.concatenate(x_tiles[n_xs:n_xs + n_bc], axis=1).astype(BF16)
            cm = jnp.concatenate(x_tiles[n_xs + n_bc:], axis=1).astype(BF16)
            zd = zbuf[g0:g0 + q, :]
            z = zd[:, :ds]
            dt = _softplus(zd[:, ds:ds + LANES] + dtb_ref[...])
            d_a = dt * a_neg2
            cs = _dot_sel_rhs(tril, d_a)
            rp = _dot_sel_lhs(cs, exp_p)
            rq = rp if q == hd else _dot_sel_lhs(cs, exp_q)
            crow = jnp.sum(jnp.where(eye_t, rq, 0.0), axis=0, keepdims=True)
            lm = jnp.exp2(jnp.where(causal_t, rq - crow, NEG_BIG))
            xdt = xs * _dot_sel_lhs(dt, exp_p)
            xb = xdt.astype(BF16)
            y_parts = []
            yo_parts = []
            for g in range(SSD_GROUPS):
                bm_g = bm[:, g * SSD_STATE:(g + 1) * SSD_STATE]
                cm_g = cm[:, g * SSD_STATE:(g + 1) * SSD_STATE]
                cb_t = _dot_nt(cm_g, jnp.concatenate([bm_g] * hpg, axis=0))
                m_g = (lm[:, g * hpg * q:(g + 1) * hpg * q] * cb_t).astype(BF16)
                for qd in range(hpg // 4):
                    kq = g * (hpg // 4) + qd
                    x_q = xb[:, kq * 4 * hd:(kq + 1) * 4 * hd]
                    bd = jnp.concatenate([x_q] * 4, axis=0) * bd_mask
                    y_parts.append(_dot(m_g[:, qd * 4 * q:(qd + 1) * 4 * q], bd))
                yo_parts.append(_dot(cm_g, st[n, g].astype(BF16)))
            y = (jnp.concatenate(y_parts, axis=1)
                 + jnp.exp2(rp) * jnp.concatenate(yo_parts, axis=1)
                 + dsk_ref[...] * xs)
            yz = y * _silu(z)
            for g in range(SSD_GROUPS):
                y_g = yz[:, g * gw:(g + 1) * gw]
                ms_g = jnp.mean(y_g * y_g, axis=-1, keepdims=True)
                y_n = y_g * lax.rsqrt(ms_g + EPS) * sng_ref[:, g * gw:(g + 1) * gw]
                mix[g0:g0 + q, dl + g * gw:dl + (g + 1) * gw] = y_n.astype(BF16)
            rp_last = rp[q - 1:q, :]
            xw = (xdt * jnp.exp2(rp_last - rp)).astype(BF16)
            for g in range(SSD_GROUPS):
                bm_g = bm[:, g * SSD_STATE:(g + 1) * SSD_STATE]
                st[n, g] = (jnp.exp2(rp_last[:, g * gw:(g + 1) * gw]) * st[n, g]
                            + _dot_tn(bm_g, xw[:, g * gw:(g + 1) * gw]))

    lext[:, 0:SUBLANES, :] = lext[:, tl:tl + SUBLANES, :]
    sext[:, 0:SUBLANES, :] = sext[:, tl:tl + SUBLANES, :]

    o = _dot(mix[...], wout_ref[...]).reshape(nb, tl, d)
    out = x_ref[...] + mod_ref[2] * o
    ms_o = jnp.mean(out * out, axis=-1, keepdims=True)
    y_ref[...] = out * lax.rsqrt(ms_o + EPS) * fng_ref[...]

    @pl.when(l_idx == n_l - 1)
    def _():
        lco_ref[...] = lext[:, 5:SUBLANES, :]
        sco_ref[...] = sext[:, 5:SUBLANES, :]
        lho_ref[...] = hc[...]
        for n in range(nb):
            for g in range(SSD_GROUPS):
                sso_ref[n, g * gw:(g + 1) * gw, :] = st[n, g].T


def _const_spec(shape):
    nd = len(shape)
    return pl.BlockSpec(shape, lambda b, l: (0,) * nd, pipeline_mode=pl.Buffered(1))


def _mixer(x, mod4, mod_block_off, weights, *, nb, tl, q, start, init=None):
    b, seq, d = x.shape
    (ng, win, lcw, lcb, wg, ba, bx, lam, scw, scb, dtb, alog, dsk, sng, wout, fng) = weights
    dl = lcw.shape[1]
    dxbc = scw.shape[1]
    ds = sng.shape[1]
    zw = win.shape[1] - 2 * dl - dxbc
    gsz = ds // SSD_GROUPS
    has_init = init is not None
    cfg = (nb, tl, q, start, has_init, d, dl, ds, dxbc, zw)
    rows = nb * tl

    in_specs = [
        pl.BlockSpec((nb, tl, d), lambda i, l: (i, l, 0)),
        pl.BlockSpec((3, nb, 1, d), lambda i, l: (0, mod_block_off + i, 0, 0)),
    ]
    args = [x, mod4]
    if has_init:
        lc0, lh0, sc0, ss0 = init
        in_specs += [
            pl.BlockSpec((nb, SUBLANES, dl), lambda i, l: (i, 0, 0)),
            pl.BlockSpec((nb, 1, dl), lambda i, l: (i, 0, 0)),
            pl.BlockSpec((nb, SUBLANES, dxbc), lambda i, l: (i, 0, 0)),
            pl.BlockSpec((nb, ds, SSD_STATE), lambda i, l: (i, 0, 0)),
        ]
        args += [lc0, lh0, sc0, ss0]
    in_specs += [_const_spec(w.shape) for w in weights]
    args += list(weights)

    out_shape = [
        jax.ShapeDtypeStruct((b, seq, d), F32),
        jax.ShapeDtypeStruct((b, CONV_W - 1, dl), F32),
        jax.ShapeDtypeStruct((b, 1, dl), F32),
        jax.ShapeDtypeStruct((b, CONV_W - 1, dxbc), F32),
        jax.ShapeDtypeStruct((b, ds, SSD_STATE), F32),
    ]
    out_specs = [
        pl.BlockSpec((nb, tl, d), lambda i, l: (i, l, 0)),
        pl.BlockSpec((nb, CONV_W - 1, dl), lambda i, l: (i, 0, 0)),
        pl.BlockSpec((nb, 1, dl), lambda i, l: (i, 0, 0)),
        pl.BlockSpec((nb, CONV_W - 1, dxbc), lambda i, l: (i, 0, 0)),
        pl.BlockSpec((nb, ds, SSD_STATE), lambda i, l: (i, 0, 0)),
    ]
    scratch = [
        pltpu.VMEM((nb, SUBLANES + tl, dl), F32),
        pltpu.VMEM((nb, SUBLANES + tl, dxbc), F32),
        pltpu.VMEM((rows, dl), F32),
        pltpu.VMEM((rows, zw), F32),
        pltpu.VMEM((rows, dl + ds), BF16),
        pltpu.VMEM((nb, 1, dl), F32),
        pltpu.VMEM((nb, SSD_GROUPS, SSD_STATE, gsz), F32),
    ]
    return pl.pallas_call(
        functools.partial(_mixer_body, cfg),
        grid=(b // nb, seq // tl),
        in_specs=in_specs,
        out_specs=out_specs,
        out_shape=out_shape,
        scratch_shapes=scratch,
        compiler_params=pltpu.CompilerParams(
            dimension_semantics=("arbitrary", "arbitrary"),
            vmem_limit_bytes=VMEM_LIMIT_BYTES,
        ),
        name="mixer_init" if has_init else "mixer",
    )(*args)


def kernel(x_prompt, x_sample, c_prompt, c_sample, state_lru_conv, state_lru_h, state_ssd_conv, state_ssd, norm_g, w_ada, b_ada, w_in, lru_conv_w, lru_conv_b, lru_w_a, lru_b_a, lru_w_x, lru_b_x, lru_lambda, ssd_conv_w, ssd_conv_b, ssd_dt_bias, ssd_a_log, ssd_d, ssd_norm_g, w_out, final_norm_g):
    depth = w_in.shape[0]
    assert depth == 1, "single-layer step"
    bp, seq_p, d = x_prompt.shape
    bs, seq_s, _ = x_sample.shape
    dl = lru_conv_w.shape[-1]
    dxbc = ssd_conv_w.shape[-1]
    ds = ssd_norm_g.shape[-1]
    hd = ds // SSD_HEADS
    in_cols = w_in.shape[-1]
    in_cols_pad = -(-in_cols // LANES) * LANES
    assert in_cols == 2 * dl + dxbc + ds + SSD_HEADS

    c_all = jnp.concatenate([c_prompt, c_sample], axis=0)
    mod = _adaln(c_all, w_ada[0], b_ada[0])
    mod4 = mod.reshape(mod.shape[0], bp + bs, 1, d)

    row = lambda v: v.reshape(1, -1)
    pad_lanes = lambda v: jnp.pad(v.reshape(1, -1), ((0, 0), (0, LANES - v.size)))
    weights = (
        row(norm_g[0]),
        jnp.pad(w_in[0], ((0, 0), (0, in_cols_pad - in_cols))).astype(BF16),
        lru_conv_w[0], row(lru_conv_b[0]),
        jnp.concatenate([lru_w_a[0], lru_w_x[0]], axis=-1).astype(BF16),
        row(lru_b_a[0]), row(lru_b_x[0]), row(lru_lambda[0]),
        ssd_conv_w[0], row(ssd_conv_b[0]),
        pad_lanes(ssd_dt_bias[0]), pad_lanes(ssd_a_log[0]),
        row(jnp.repeat(ssd_d[0], hd)), row(ssd_norm_g[0]),
        w_out[0].astype(BF16), row(final_norm_g),
    )

    nb_s = 8
    assert bp % nb_s == 0 and bs % nb_s == 0
    tl_p = 4 * CHUNK
    assert seq_p % tl_p == 0
    y_p, lc_p, lh_p, sc_p, ss_p = _mixer(
        x_prompt, mod4, 0, weights, nb=1, tl=tl_p, q=CHUNK, start=0)
    pad_tail = lambda v: jnp.pad(v, ((0, 0), (SUBLANES - (CONV_W - 1), 0), (0, 0)))
    init_s = (
        pad_tail(state_lru_conv[0]),
        state_lru_h[0].reshape(bs, 1, dl),
        pad_tail(state_ssd_conv[0]),
        state_ssd[0].reshape(bs, ds, SSD_STATE),
    )
    y_s, lc_s, lh_s, sc_s, ss_s = _mixer(
        x_sample, mod4, bp // nb_s, weights, nb=nb_s, tl=seq_s, q=seq_s, start=PAST_LEN, init=init_s)

    st_shape = (1,) + state_ssd.shape[2:]
    return (
        y_p, y_s,
        lc_p[None], lh_p.reshape(1, bp, dl), sc_p[None], ss_p.reshape((1, bp) + st_shape[1:]),
        lc_s[None], lh_s.reshape(1, bs, dl), sc_s[None], ss_s.reshape((1, bs) + st_shape[1:]),
    )
```

```python
import functools

import jax
import jax.numpy as jnp
from jax import lax
from jax.experimental import pallas as pl
from jax.experimental.pallas import tpu as pltpu

F32 = jnp.float32
BF16 = jnp.bfloat16

CONV_W = 4
LRU_HEADS = 8
LRU_C = 8.0
SSD_HEADS = 16
SSD_GROUPS = 2
SSD_STATE = 128
CHUNK = 64
PAST_LEN = 1024
EPS = 1e-6

LANES = 128
SUBLANES = 8
VMEM_LIMIT_BYTES = 56 * 1024 * 1024
HOIST_RANGE = 4096
PIECE_COLS = 256
SSD_TICKS = 4

NEG_BIG = -1e30
LOG2E = 1.4426950408889634
LN2 = 0.6931471805599453


def _dot(a, b):
    return jnp.dot(a, b, preferred_element_type=F32)


def _dot_nt(a, b):
    return lax.dot_general(a, b, (((1,), (1,)), ((), ())), preferred_element_type=F32)


def _dot_tn(a, b):
    return lax.dot_general(a, b, (((0,), (0,)), ((), ())), preferred_element_type=F32)


def _split3(x):
    hi = x.astype(BF16)
    r1 = x - hi.astype(F32)
    mid = r1.astype(BF16)
    lo = (r1 - mid.astype(F32)).astype(BF16)
    return hi, mid, lo


def _dot_sel_rhs(sel, x):
    hi, mid, lo = _split3(x)
    return _dot(sel, hi) + _dot(sel, mid) + _dot(sel, lo)


def _dot_sel_lhs(x, sel):
    hi, mid, lo = _split3(x)
    return _dot(hi, sel) + _dot(mid, sel) + _dot(lo, sel)


def _sigmoid(x):
    return 1.0 / (1.0 + jnp.exp2(x * (-LOG2E)))


def _silu(x):
    hx = 0.5 * x
    return hx * jnp.tanh(hx) + hx


def _causal_conv_tiles(ext_ref, n, r0, q, w_ref, b_ref, sub_rows):
    c = ext_ref.shape[-1]
    tiles = []
    for t in range(c // LANES):
        ls = slice(t * LANES, (t + 1) * LANES)
        w = [w_ref[k:k + 1, ls] for k in range(CONV_W)]
        bias = b_ref[:, ls]
        slab = ext_ref[n, r0:r0 + SUBLANES, ls]
        prev = [pltpu.roll(slab, CONV_W - 1 - k, 0) for k in range(CONV_W - 1)]
        outs = []
        for j in range(q // SUBLANES):
            lo = r0 + (j + 1) * SUBLANES
            slab = ext_ref[n, lo:lo + SUBLANES, ls]
            cur = [pltpu.roll(slab, CONV_W - 1 - k, 0) for k in range(CONV_W - 1)]
            acc = bias + w[CONV_W - 1] * slab
            for k in range(CONV_W - 1):
                acc = acc + w[k] * jnp.where(sub_rows < CONV_W - 1 - k, prev[k], cur[k])
            outs.append(acc)
            prev = cur
        tiles.append(jnp.concatenate(outs, axis=0))
    return tiles


def _softplus(x):
    return jnp.maximum(x, 0.0) + jnp.log1p(jnp.exp(-jnp.abs(x)))


def _adaln_body(c_ref, w_ref, b_ref, o_ref):
    a = _silu(c_ref[...]).astype(BF16)
    o_ref[0] = _dot(a, w_ref[...].astype(BF16)) + b_ref[...]


def _adaln(c_all, w_ada, b_ada):
    rows, d = c_all.shape
    parts = w_ada.shape[1] // d
    return pl.pallas_call(
        _adaln_body,
        grid=(parts,),
        in_specs=[
            pl.BlockSpec((rows, d), lambda j: (0, 0)),
            pl.BlockSpec((d, d), lambda j: (0, j)),
            pl.BlockSpec((1, d), lambda j: (0, j)),
        ],
        out_specs=pl.BlockSpec((1, rows, d), lambda j: (j, 0, 0)),
        out_shape=jax.ShapeDtypeStruct((parts, rows, d), F32),
        name="adaln",
    )(c_all, w_ada, b_ada.reshape(1, -1))


def _mixer_body(cfg, *refs):
    (nb, tl, q, start, has_init, skew, n_tiles, n_l, d, dl, ds, dxbc, zw) = cfg
    hd = ds // SSD_HEADS
    gw = ds // SSD_GROUPS
    lb = dl // LRU_HEADS
    hpg = SSD_HEADS // SSD_GROUPS
    n_chunks = tl // q
    rows = nb * tl
    n_set = 2 if skew else 1
    lag_mid = 1 if skew else 0

    it = iter(refs)
    x1_ref = next(it)
    x3_ref = next(it) if skew else x1_ref
    mod1_ref = next(it)
    mod3_ref = next(it) if skew else mod1_ref
    if has_init:
        lc0_ref, lh0_ref, sc0_ref, ss0_ref = next(it), next(it), next(it), next(it)
    (ng_ref, win_ref, lcw_ref, lcb_ref, wg_ref, ba_ref, bx_ref, lam_ref, scw_ref, scb_ref,
     dtb_ref, alog_ref, dsk_ref, sng_ref, wout_ref, fng_ref) = [next(it) for _ in range(16)]
    y_ref, lco_ref, lho_ref, sco_ref, sso_ref = [next(it) for _ in range(5)]
    sets = [tuple(next(it) for _ in range(4)) for _ in range(n_set)]
    mixes = [next(it) for _ in range(n_set)]
    hbuf, obuf, tail_l, tail_s, hc, st = [next(it) for _ in range(6)]

    step = pl.program_id(0)
    t_mid = jnp.clip(step - lag_mid, 0, n_tiles - 1)
    l_mid = lax.rem(t_mid, n_l)

    if skew:
        @pl.when(step == 0)
        def _():
            for buf in sets[1]:
                buf[...] = jnp.zeros(buf.shape, buf.dtype)
            mixes[0][...] = jnp.zeros(mixes[0].shape, mixes[0].dtype)

    @pl.when(l_mid == 0)
    def _():
        if has_init:
            tail_l[...] = lc0_ref[...]
            tail_s[...] = sc0_ref[...]
            hc[...] = lh0_ref[...]
            for n in range(nb):
                for g in range(SSD_GROUPS):
                    st[n, g] = ss0_ref[n, g * gw:(g + 1) * gw, :].T
        else:
            tail_l[...] = jnp.zeros(tail_l.shape, F32)
            tail_s[...] = jnp.zeros(tail_s.shape, F32)
            hc[...] = jnp.zeros(hc.shape, F32)
            st[...] = jnp.zeros(st.shape, F32)

    def col_pieces(width):
        return [(lo, min(lo + PIECE_COLS, width)) for lo in range(0, width, PIECE_COLS)]

    def stage_in(bufs):
        lext, sext, gbuf, zbuf = bufs

        def p_prep():
            x = x1_ref[...]
            ms = jnp.mean(x * x, axis=-1, keepdims=True)
            hn = x * lax.rsqrt(ms + EPS) * (ng_ref[...] * (1.0 + mod1_ref[1])) + mod1_ref[0]
            hbuf[...] = hn.reshape(rows, d).astype(BF16)

        def p_ext(ext, base, lo, hi):
            ext[:, SUBLANES:SUBLANES + tl, lo:hi] = _dot(
                hbuf[...], win_ref[:, base + lo:base + hi]).reshape(nb, tl, hi - lo)

        def p_flat(buf, base, lo, hi):
            buf[:, lo:hi] = _dot(hbuf[...], win_ref[:, base + lo:base + hi])

        pieces = [p_prep]
        pieces += [functools.partial(p_ext, lext, 0, lo, hi) for lo, hi in col_pieces(dl)]
        pieces += [functools.partial(p_flat, gbuf, dl, lo, hi) for lo, hi in col_pieces(dl)]
        pieces += [functools.partial(p_ext, sext, 2 * dl, lo, hi) for lo, hi in col_pieces(dxbc)]
        pieces += [functools.partial(p_flat, zbuf, 2 * dl + dxbc, lo, hi) for lo, hi in col_pieces(zw)]
        return pieces

    def stage_out(mix):
        def p_proj(lo, hi):
            obuf[:, lo:hi] = _dot(mix[...], wout_ref[:, lo:hi])

        def p_norm():
            out = x3_ref[...] + mod3_ref[2] * obuf[...].reshape(nb, tl, d)
            ms_o = jnp.mean(out * out, axis=-1, keepdims=True)
            y_ref[...] = out * lax.rsqrt(ms_o + EPS) * fng_ref[...]

        return [functools.partial(p_proj, lo, hi) for lo, hi in col_pieces(d)] + [p_norm]

    def stage_mid(bufs, mix, pieces=()):
        lext, sext, gbuf, zbuf = bufs
        n_ticks = nb * n_chunks * (LRU_HEADS + SSD_TICKS)
        progress = {"ticks": 0, "done": 0}

        def tick():
            progress["ticks"] += 1
            target = progress["ticks"] * len(pieces) // n_ticks
            while progress["done"] < target:
                pieces[progress["done"]]()
                progress["done"] += 1
        lext[:, 0:SUBLANES, :] = tail_l[...]
        sext[:, 0:SUBLANES, :] = tail_s[...]

        cneg2 = (-LRU_C * LOG2E) * _softplus(-lam_ref[...])
        a_neg2 = (-LOG2E) * jnp.exp(alog_ref[...])
        r_qq = lax.broadcasted_iota(jnp.int32, (q, q), 0)
        c_qq = lax.broadcasted_iota(jnp.int32, (q, q), 1)
        tril = jnp.where(r_qq >= c_qq, 1.0, 0.0).astype(BF16)
        e_r = lax.broadcasted_iota(jnp.int32, (LANES, ds), 0)
        e_c = lax.broadcasted_iota(jnp.int32, (LANES, ds), 1)
        exp_p = jnp.where(e_r == e_c // hd, 1.0, 0.0).astype(BF16)
        if q == hd:
            exp_q = exp_p
        else:
            eq_r = lax.broadcasted_iota(jnp.int32, (LANES, SSD_HEADS * q), 0)
            eq_c = lax.broadcasted_iota(jnp.int32, (LANES, SSD_HEADS * q), 1)
            exp_q = jnp.where(eq_r == eq_c // q, 1.0, 0.0).astype(BF16)
        row_t = lax.broadcasted_iota(jnp.int32, (q, SSD_HEADS * q), 0)
        lane_t = lax.broadcasted_iota(jnp.int32, (q, SSD_HEADS * q), 1) % q
        eye_t = lane_t == row_t
        causal_t = lane_t <= row_t
        bd_r = lax.broadcasted_iota(jnp.int32, (4 * q, 4 * hd), 0) // q
        bd_c = lax.broadcasted_iota(jnp.int32, (4 * q, 4 * hd), 1) // hd
        bd_mask = jnp.where(bd_r == bd_c, 1.0, 0.0).astype(BF16)
        sub_rows = lax.broadcasted_iota(jnp.int32, (SUBLANES, lb), 0)
        first_row = lax.broadcasted_iota(jnp.int32, (q, lb), 0) == 0

        for n in range(nb):
            for c in range(n_chunks):
                r0 = c * q
                g0 = n * tl + r0

                u_tiles = _causal_conv_tiles(lext, n, r0, q, lcw_ref, lcb_ref, sub_rows)
                for h in range(LRU_HEADS):
                    hs = slice(h * lb, (h + 1) * lb)
                    u_h = u_tiles[h]
                    gates = _dot(u_h.astype(BF16), wg_ref[h])
                    r_g = _sigmoid(gates[:, :lb] + ba_ref[:, hs])
                    i_g = _sigmoid(gates[:, lb:] + bx_ref[:, hs])
                    log2_a = cneg2[:, hs] * r_g
                    a = jnp.exp2(log2_a)
                    m2 = jnp.tanh(log2_a * (-LN2)) * (a * a + 1.0)
                    mult = jnp.where(m2 > 0.0, m2 * lax.rsqrt(m2), 0.0)
                    if start == 0 and c == 0:
                        mult = jnp.where(jnp.logical_and(first_row, l_mid == 0), 1.0, mult)
                    bv = mult * (i_g * u_h)
                    hp = hc[n, :, hs]
                    blocks = []
                    for j in range(q // SUBLANES):
                        a_b = a[j * SUBLANES:(j + 1) * SUBLANES]
                        b_b = bv[j * SUBLANES:(j + 1) * SUBLANES]
                        for k in (1, 2, 4):
                            keep = sub_rows >= k
                            a_s = jnp.where(keep, pltpu.roll(a_b, k, 0), 1.0)
                            b_s = jnp.where(keep, pltpu.roll(b_b, k, 0), 0.0)
                            b_b = a_b * b_s + b_b
                            a_b = a_b * a_s
                        h_b = a_b * hp + b_b
                        hp = h_b[SUBLANES - 1:SUBLANES]
                        blocks.append(h_b)
                    hc[n, :, hs] = hp
                    h_all = jnp.concatenate(blocks, axis=0)
                    y_l = h_all * _silu(gbuf[g0:g0 + q, hs])
                    mix[g0:g0 + q, hs] = y_l.astype(BF16)
                    tick()

                x_tiles = [_silu(t) for t in _causal_conv_tiles(sext, n, r0, q, scw_ref, scb_ref, sub_rows)]
                n_xs = ds // LANES
                n_bc = SSD_GROUPS * SSD_STATE // LANES
                xs = jnp.concatenate(x_tiles[:n_xs], axis=1)
                bm = jnp.concatenate(x_tiles[n_xs:n_xs + n_bc], axis=1).astype(BF16)
                cm = jnp.concatenate(x_tiles[n_xs + n_bc:], axis=1).astype(BF16)
                tick()
                zd = zbuf[g0:g0 + q, :]
                z = zd[:, :ds]
                dt = _softplus(zd[:, ds:ds + LANES] + dtb_ref[...])
                d_a = dt * a_neg2
                cs = _dot_sel_rhs(tril, d_a)
                rp = _dot_sel_lhs(cs, exp_p)
                rq = rp if q == hd else _dot_sel_lhs(cs, exp_q)
                crow = jnp.sum(jnp.where(eye_t, rq, 0.0), axis=0, keepdims=True)
                lm = jnp.exp2(jnp.where(causal_t, rq - crow, NEG_BIG))
                tick()
                xdt = xs * _dot_sel_lhs(dt, exp_p)
                xb = xdt.astype(BF16)
                y_parts = []
                yo_parts = []
                for g in range(SSD_GROUPS):
                    bm_g = bm[:, g * SSD_STATE:(g + 1) * SSD_STATE]
                    cm_g = cm[:, g * SSD_STATE:(g + 1) * SSD_STATE]
                    cb_t = _dot_nt(cm_g, jnp.concatenate([bm_g] * hpg, axis=0))
                    m_g = (lm[:, g * hpg * q:(g + 1) * hpg * q] * cb_t).astype(BF16)
                    for qd in range(hpg // 4):
                        kq = g * (hpg // 4) + qd
                        x_q = xb[:, kq * 4 * hd:(kq + 1) * 4 * hd]
                        bd = jnp.concatenate([x_q] * 4, axis=0) * bd_mask
                        y_parts.append(_dot(m_g[:, qd * 4 * q:(qd + 1) * 4 * q], bd))
                    yo_parts.append(_dot(cm_g, st[n, g].astype(BF16)))
                tick()
                y = (jnp.concatenate(y_parts, axis=1)
                     + jnp.exp2(rp) * jnp.concatenate(yo_parts, axis=1)
                     + dsk_ref[...] * xs)
                yz = y * _silu(z)
                for g in range(SSD_GROUPS):
                    y_g = yz[:, g * gw:(g + 1) * gw]
                    ms_g = jnp.mean(y_g * y_g, axis=-1, keepdims=True)
                    y_n = y_g * lax.rsqrt(ms_g + EPS) * sng_ref[:, g * gw:(g + 1) * gw]
                    mix[g0:g0 + q, dl + g * gw:dl + (g + 1) * gw] = y_n.astype(BF16)
                rp_last = rp[q - 1:q, :]
                xw = (xdt * jnp.exp2(rp_last - rp)).astype(BF16)
                for g in range(SSD_GROUPS):
                    bm_g = bm[:, g * SSD_STATE:(g + 1) * SSD_STATE]
                    st[n, g] = (jnp.exp2(rp_last[:, g * gw:(g + 1) * gw]) * st[n, g]
                                + _dot_tn(bm_g, xw[:, g * gw:(g + 1) * gw]))

                tick()
        assert progress["done"] == len(pieces)

        tail_l[...] = lext[:, tl:tl + SUBLANES, :]
        tail_s[...] = sext[:, tl:tl + SUBLANES, :]

    if skew:
        parity = lax.rem(step, 2)
        for p in range(2):
            @pl.when(parity == p)
            def _(p=p):
                stage_mid(sets[1 - p], mixes[1 - p], stage_out(mixes[p]) + stage_in(sets[p]))
    else:
        for piece in stage_in(sets[0]):
            piece()
        stage_mid(sets[0], mixes[0])
        for piece in stage_out(mixes[0]):
            piece()

    mid_valid = jnp.logical_and(step >= lag_mid, step < n_tiles + lag_mid)

    @pl.when(jnp.logical_and(l_mid == n_l - 1, mid_valid))
    def _():
        lco_ref[...] = tail_l[:, 5:SUBLANES, :]
        sco_ref[...] = tail_s[:, 5:SUBLANES, :]
        lho_ref[...] = hc[...]
        for n in range(nb):
            for g in range(SSD_GROUPS):
                sso_ref[n, g * gw:(g + 1) * gw, :] = st[n, g].T


def _const_spec(shape):
    nd = len(shape)
    return pl.BlockSpec(shape, lambda s: (0,) * nd, pipeline_mode=pl.Buffered(1))


def _mixer(x, mod4, mod_block_off, weights, *, nb, tl, q, start, skew, init=None):
    b, seq, d = x.shape
    (ng, win, lcw, lcb, wg, ba, bx, lam, scw, scb, dtb, alog, dsk, sng, wout, fng) = weights
    dl = lcw.shape[1]
    dxbc = scw.shape[1]
    ds = sng.shape[1]
    zw = win.shape[1] - 2 * dl - dxbc
    gsz = ds // SSD_GROUPS
    has_init = init is not None
    n_l = seq // tl
    n_tiles = (b // nb) * n_l
    lag_mid, lag_out = (1, 2) if skew else (0, 0)
    cfg = (nb, tl, q, start, has_init, skew, n_tiles, n_l, d, dl, ds, dxbc, zw)
    rows = nb * tl

    def tile(s, lag):
        t = jnp.clip(s - lag, 0, n_tiles - 1)
        return t // n_l, t % n_l

    x_spec = lambda lag: pl.BlockSpec((nb, tl, d), lambda s: (*tile(s, lag), 0))
    mod_spec = lambda lag: pl.BlockSpec((3, nb, 1, d), lambda s: (0, mod_block_off + tile(s, lag)[0], 0, 0))
    per_block = lambda shape: pl.BlockSpec(shape, lambda s: (tile(s, lag_mid)[0], 0, 0))

    in_specs = [x_spec(0)] + ([x_spec(lag_out)] if skew else [])
    args = [x] + ([x] if skew else [])
    in_specs += [mod_spec(0)] + ([mod_spec(lag_out)] if skew else [])
    args += [mod4] + ([mod4] if skew else [])
    if has_init:
        in_specs += [per_block((nb, SUBLANES, dl)), per_block((nb, 1, dl)),
                     per_block((nb, SUBLANES, dxbc)), per_block((nb, ds, SSD_STATE))]
        args += list(init)
    in_specs += [_const_spec(w.shape) for w in weights]
    args += list(weights)

    out_shape = [
        jax.ShapeDtypeStruct((b, seq, d), F32),
        jax.ShapeDtypeStruct((b, CONV_W - 1, dl), F32),
        jax.ShapeDtypeStruct((b, 1, dl), F32),
        jax.ShapeDtypeStruct((b, CONV_W - 1, dxbc), F32),
        jax.ShapeDtypeStruct((b, ds, SSD_STATE), F32),
    ]
    out_specs = [
        x_spec(lag_out),
        per_block((nb, CONV_W - 1, dl)), per_block((nb, 1, dl)),
        per_block((nb, CONV_W - 1, dxbc)), per_block((nb, ds, SSD_STATE)),
    ]
    n_set = 2 if skew else 1
    scratch = []
    for _ in range(n_set):
        scratch += [
            pltpu.VMEM((nb, SUBLANES + tl, dl), F32),
            pltpu.VMEM((nb, SUBLANES + tl, dxbc), F32),
            pltpu.VMEM((rows, dl), F32),
            pltpu.VMEM((rows, zw), F32),
        ]
    scratch += [pltpu.VMEM((rows, dl + ds), BF16) for _ in range(n_set)]
    scratch += [
        pltpu.VMEM((rows, d), BF16),
        pltpu.VMEM((rows, d), F32),
        pltpu.VMEM((nb, SUBLANES, dl), F32),
        pltpu.VMEM((nb, SUBLANES, dxbc), F32),
        pltpu.VMEM((nb, 1, dl), F32),
        pltpu.VMEM((nb, SSD_GROUPS, SSD_STATE, gsz), F32),
    ]
    return pl.pallas_call(
        functools.partial(_mixer_body, cfg),
        grid=(n_tiles + lag_out,),
        in_specs=in_specs,
        out_specs=out_specs,
        out_shape=out_shape,
        scratch_shapes=scratch,
        compiler_params=pltpu.CompilerParams(
            dimension_semantics=("arbitrary",),
            vmem_limit_bytes=VMEM_LIMIT_BYTES,
        ),
        name="mixer_init" if has_init else "mixer",
    )(*args)


def kernel(x_prompt, x_sample, c_prompt, c_sample, state_lru_conv, state_lru_h, state_ssd_conv, state_ssd, norm_g, w_ada, b_ada, w_in, lru_conv_w, lru_conv_b, lru_w_a, lru_b_a, lru_w_x, lru_b_x, lru_lambda, ssd_conv_w, ssd_conv_b, ssd_dt_bias, ssd_a_log, ssd_d, ssd_norm_g, w_out, final_norm_g):
    depth = w_in.shape[0]
    assert depth == 1, "single-layer step"
    bp, seq_p, d = x_prompt.shape
    bs, seq_s, _ = x_sample.shape
    dl = lru_conv_w.shape[-1]
    dxbc = ssd_conv_w.shape[-1]
    ds = ssd_norm_g.shape[-1]
    hd = ds // SSD_HEADS
    in_cols = w_in.shape[-1]
    in_cols_pad = -(-in_cols // LANES) * LANES
    assert in_cols == 2 * dl + dxbc + ds + SSD_HEADS

    c_all = jnp.concatenate([c_prompt, c_sample], axis=0)
    mod = _adaln(c_all, w_ada[0], b_ada[0])
    mod4 = mod.reshape(mod.shape[0], bp + bs, 1, d)

    row = lambda v: v.reshape(1, -1)
    pad_lanes = lambda v: jnp.pad(v.reshape(1, -1), ((0, 0), (0, LANES - v.size)))
    weights = (
        row(norm_g[0]),
        jnp.pad(w_in[0], ((0, 0), (0, in_cols_pad - in_cols))).astype(BF16),
        lru_conv_w[0], row(lru_conv_b[0]),
        jnp.concatenate([lru_w_a[0], lru_w_x[0]], axis=-1).astype(BF16),
        row(lru_b_a[0]), row(lru_b_x[0]), row(lru_lambda[0]),
        ssd_conv_w[0], row(ssd_conv_b[0]),
        pad_lanes(ssd_dt_bias[0]), pad_lanes(ssd_a_log[0]),
        row(jnp.repeat(ssd_d[0], hd)), row(ssd_norm_g[0]),
        w_out[0].astype(BF16), row(final_norm_g),
    )

    nb_s = 8
    assert bp % nb_s == 0 and bs % nb_s == 0
    tl_p = 4 * CHUNK
    assert seq_p % tl_p == 0
    y_p, lc_p, lh_p, sc_p, ss_p = _mixer(
        x_prompt, mod4, 0, weights, nb=1, tl=tl_p, q=CHUNK, start=0, skew=True)
    pad_tail = lambda v: jnp.pad(v, ((0, 0), (SUBLANES - (CONV_W - 1), 0), (0, 0)))
    init_s = (
        pad_tail(state_lru_conv[0]),
        state_lru_h[0].reshape(bs, 1, dl),
        pad_tail(state_ssd_conv[0]),
        state_ssd[0].reshape(bs, ds, SSD_STATE),
    )
    y_s, lc_s, lh_s, sc_s, ss_s = _mixer(
        x_sample, mod4, bp // nb_s, weights, nb=nb_s, tl=seq_s, q=seq_s, start=PAST_LEN, skew=False,
        init=init_s)

    st_shape = (1,) + state_ssd.shape[2:]
    return (
        y_p, y_s,
        lc_p[None], lh_p.reshape(1, bp, dl), sc_p[None], ss_p.reshape((1, bp) + st_shape[1:]),
        lc_s[None], lh_s.reshape(1, bs, dl), sc_s[None], ss_s.reshape((1, bs) + st_shape[1:]),
    )
```
